```python
import jax, jax.numpy as jnp
from jax import lax
import numpy as np

D_MODEL = 2048
BATCH = 4
SEQ = 4096
DEPTH = 1

GRID_W = 64
CTX_LEN = 256
MLA_HEADS = 8
Q_LORA = 512
KV_LORA = 256
NOPE_DIM = 128
ROPE_DIM = 64
V_DIM = 128
QK_DIM = NOPE_DIM + ROPE_DIM
ROPE_THETA = 10000.0
Q_BLOCK = 128
CHUNK = 128
SGU_GROUPS = 8
SGU_WIDTH = 1024
SGU_GROUP_DIM = SGU_WIDTH // SGU_GROUPS
D_FF = ((8 * D_MODEL // 3 + 255) // 256) * 256
N_BRANCH = 2
N_MOD = 6
EPS = 1e-6
OFF_KVC = Q_LORA
OFF_U = OFF_KVC + KV_LORA + ROPE_DIM
OFF_V = OFF_U + SGU_WIDTH
OFF_GATE = OFF_V + SGU_WIDTH
IN_COLS = OFF_GATE + N_BRANCH * D_MODEL

kernel_name = "hybrid_mla_sgu_prefix_dit_block"


def _rms(x, g):
    xf = x.astype(jnp.float32)
    y = xf * lax.rsqrt(jnp.mean(xf * xf, axis=-1, keepdims=True) + EPS)
    return (y * g.astype(jnp.float32)).astype(x.dtype)


def _modulate(h, shift, scale):
    return h * (1 + scale) + shift


def _axial_angles(n):
    rows = n // GRID_W
    row = jnp.repeat(jnp.arange(rows, dtype=jnp.float32), GRID_W)
    col = jnp.tile(jnp.arange(GRID_W, dtype=jnp.float32), rows)
    nf = ROPE_DIM // 4
    freqs = ROPE_THETA ** (-jnp.arange(nf, dtype=jnp.float32) / nf)
    return row[:, None] * freqs[None, :], col[:, None] * freqs[None, :]


def _rotate(t, ang):
    t1, t2 = jnp.split(t, 2, axis=-1)
    cos = jnp.cos(ang)[:, None, :]
    sin = jnp.sin(ang)[:, None, :]
    return jnp.concatenate([t1 * cos - t2 * sin, t2 * cos + t1 * sin], axis=-1)


def _rope_tail(t, ang_r, ang_c):
    nope, rope = t[..., :NOPE_DIM], t[..., NOPE_DIM:].astype(jnp.float32)
    rot = jnp.concatenate([_rotate(rope[..., :ROPE_DIM // 2], ang_r),
                           _rotate(rope[..., ROPE_DIM // 2:], ang_c)], axis=-1)
    return jnp.concatenate([nope, rot.astype(t.dtype)], axis=-1)


def _mla_queries(qc, q_norm_g, w_uq, qk_norm_q):
    q = _rms(qc, q_norm_g) @ w_uq
    q = q.reshape(q.shape[:-1] + (MLA_HEADS, QK_DIM))
    return _rms(q, qk_norm_q)


def _mla_keys_values(kv_in, kv_norm_g, w_ukv, qk_norm_k):
    kvc, k_rope = kv_in[..., :KV_LORA], kv_in[..., KV_LORA:]
    kv = _rms(kvc, kv_norm_g) @ w_ukv
    kv = kv.reshape(kv.shape[:-1] + (MLA_HEADS, NOPE_DIM + V_DIM))
    k_nope, v = kv[..., :NOPE_DIM], kv[..., NOPE_DIM:]
    k_rope = jnp.broadcast_to(k_rope[..., None, :], k_nope.shape[:-1] + (ROPE_DIM,))
    k = _rms(jnp.concatenate([k_nope, k_rope], axis=-1), qk_norm_k)
    return k, v


def _attention(q, k, v):
    b, n = q.shape[0], q.shape[1]
    nblk = n // Q_BLOCK
    qb = q.reshape(b, nblk, Q_BLOCK, MLA_HEADS, QK_DIM).transpose(1, 0, 2, 3, 4)
    scale = QK_DIM ** -0.5

    def one_block(qi):
        s = jnp.einsum('bqhd,bkhd->bhqk', qi, k, preferred_element_type=jnp.float32) * scale
        p = jax.nn.softmax(s, axis=-1)
        o = jnp.einsum('bhqk,bkhd->bqhd', p.astype(v.dtype), v, preferred_element_type=jnp.float32)
        return o.astype(v.dtype)

    o = lax.map(one_block, qb)
    return o.transpose(1, 0, 2, 3, 4).reshape(b, n, MLA_HEADS * V_DIM)


def _sgu(u_in, v_in, norm_g, norm_b, w_s, b_s):
    u = jax.nn.gelu(u_in, approximate=False)
    v = jax.nn.gelu(v_in, approximate=False)
    vf = v.astype(jnp.float32)
    mu = jnp.mean(vf, axis=-1, keepdims=True)
    var = jnp.mean(jnp.square(vf - mu), axis=-1, keepdims=True)
    vn = ((vf - mu) * lax.rsqrt(var + EPS) * norm_g.astype(jnp.float32)
          + norm_b.astype(jnp.float32)).astype(v.dtype)
    b, n = v.shape[0], v.shape[1]
    nc = n // CHUNK
    vs = vn.reshape(b, nc, CHUNK, SGU_GROUPS, SGU_GROUP_DIM)
    mixed = jnp.einsum('gij,bnjgc->bnigc', w_s, vs) + b_s.T[:, :, None]
    out = u.reshape(b, nc, CHUNK, SGU_GROUPS, SGU_GROUP_DIM) * mixed
    return out.reshape(b, n, SGU_WIDTH)


def _merge(attn_o, sgu_o, gate_in, w_br_attn, w_br_sgu, w_out):
    g = jax.nn.sigmoid(gate_in.astype(jnp.float32)).astype(attn_o.dtype)
    merged = g[..., :D_MODEL] * (attn_o @ w_br_attn) + g[..., D_MODEL:] * (sgu_o @ w_br_sgu)
    return merged @ w_out


def _swiglu(h, w_ffn_in, w_ffn_out):
    a, b = jnp.split(h @ w_ffn_in, 2, axis=-1)
    return (jax.nn.silu(a) * b) @ w_ffn_out


def setup_inputs(seed: int = 0) -> dict:
    key = jax.random.key(seed)
    ks = jax.random.split(key, 26)
    f32 = jnp.float32

    def dense(k, shape, fan_in, gain=1.0):
        return jax.random.normal(k, shape, f32) * (gain * fan_in ** -0.5)

    def gain_vec(k, shape):
        return 1.0 + 0.02 * jax.random.normal(k, shape, f32)

    def bias_vec(k, shape):
        return 0.02 * jax.random.normal(k, shape, f32)

    L = DEPTH
    return {
        "x": jax.random.normal(ks[0], (BATCH, SEQ, D_MODEL), f32),
        "c": jax.random.normal(ks[1], (BATCH, D_MODEL), f32),
        "ctx": jax.random.normal(ks[2], (BATCH, CTX_LEN, D_MODEL), f32),
        "c_ctx": jax.random.normal(ks[3], (D_MODEL,), f32),
        "w_mod": dense(ks[4], (L, D_MODEL, N_MOD * D_MODEL), D_MODEL, 0.5),
        "b_mod": bias_vec(ks[5], (L, N_MOD * D_MODEL)),
        "norm1_g": gain_vec(ks[6], (L, D_MODEL)),
        "w_in": dense(ks[7], (L, D_MODEL, IN_COLS), D_MODEL),
        "q_norm_g": gain_vec(ks[8], (L, Q_LORA)),
        "kv_norm_g": gain_vec(ks[9], (L, KV_LORA)),
        "w_uq": dense(ks[10], (L, Q_LORA, MLA_HEADS * QK_DIM), Q_LORA),
        "w_ukv": dense(ks[11], (L, KV_LORA, MLA_HEADS * (NOPE_DIM + V_DIM)), KV_LORA),
        "qk_norm_q": gain_vec(ks[12], (L, QK_DIM)),
        "qk_norm_k": gain_vec(ks[13], (L, QK_DIM)),
        "sgu_norm_g": gain_vec(ks[14], (L, SGU_WIDTH)),
        "sgu_norm_b": bias_vec(ks[15], (L, SGU_WIDTH)),
        "w_spatial": dense(ks[16], (L, SGU_GROUPS, CHUNK, CHUNK), CHUNK),
        "b_spatial": gain_vec(ks[17], (L, SGU_GROUPS, CHUNK)),
        "w_br_attn": dense(ks[18], (L, MLA_HEADS * V_DIM, D_MODEL), MLA_HEADS * V_DIM),
        "w_br_sgu": dense(ks[19], (L, SGU_WIDTH, D_MODEL), SGU_WIDTH),
        "w_out": dense(ks[20], (L, D_MODEL, D_MODEL), D_MODEL),
        "norm2_g": gain_vec(ks[21], (L, D_MODEL)),
        "w_ffn_in": dense(ks[22], (L, D_MODEL, 2 * D_FF), D_MODEL),
        "w_ffn_out": dense(ks[23], (L, D_FF, D_MODEL), D_FF),
    }


def reference(x, c, ctx, c_ctx, w_mod, b_mod, norm1_g, w_in, q_norm_g, kv_norm_g, w_uq,
              w_ukv, qk_norm_q, qk_norm_k, sgu_norm_g, sgu_norm_b, w_spatial, b_spatial,
              w_br_attn, w_br_sgu, w_out, norm2_g, w_ffn_in, w_ffn_out):
    n = x.shape[1]
    ang_r, ang_c = _axial_angles(n)
    silu_c = jax.nn.silu(c)
    silu_cc = jax.nn.silu(c_ctx)
    for l in range(DEPTH):
        mod = silu_c @ w_mod[l] + b_mod[l]
        sh1, sc1, g1, sh2, sc2, g2 = [m[:, None, :] for m in jnp.split(mod, N_MOD, axis=-1)]
        mod_c = silu_cc @ w_mod[l][:, :2 * D_MODEL] + b_mod[l][:2 * D_MODEL]
        sh1c, sc1c = jnp.split(mod_c, 2)
        ctx_h = _modulate(_rms(ctx, norm1_g[l]), sh1c, sc1c)
        k_ctx, v_ctx = _mla_keys_values(ctx_h @ w_in[l][:, OFF_KVC:OFF_U],
                                        kv_norm_g[l], w_ukv[l], qk_norm_k[l])

        h = _modulate(_rms(x, norm1_g[l]), sh1, sc1)
        proj = h @ w_in[l]
        q = _rope_tail(_mla_queries(proj[..., :OFF_KVC], q_norm_g[l], w_uq[l], qk_norm_q[l]),
                       ang_r, ang_c)
        k_lat, v_lat = _mla_keys_values(proj[..., OFF_KVC:OFF_U], kv_norm_g[l], w_ukv[l],
                                        qk_norm_k[l])
        k_lat = _rope_tail(k_lat, ang_r, ang_c)
        attn_o = _attention(q, jnp.concatenate([k_lat, k_ctx], axis=1),
                            jnp.concatenate([v_lat, v_ctx], axis=1))
        sgu_o = _sgu(proj[..., OFF_U:OFF_V], proj[..., OFF_V:OFF_GATE], sgu_norm_g[l],
                     sgu_norm_b[l], w_spatial[l], b_spatial[l])
        x_new = x + g1 * _merge(attn_o, sgu_o, proj[..., OFF_GATE:], w_br_attn[l],
                                w_br_sgu[l], w_out[l])
        h2 = _modulate(_rms(x_new, norm2_g[l]), sh2, sc2)
        x_new = x_new + g2 * _swiglu(h2, w_ffn_in[l], w_ffn_out[l])

        if l + 1 < DEPTH:
            mod_r = silu_cc @ w_mod[l][:, 2 * D_MODEL:] + b_mod[l][2 * D_MODEL:]
            g1c, sh2c, sc2c, g2c = jnp.split(mod_r, 4)
            proj_c = ctx_h @ w_in[l]
            q_c = _mla_queries(proj_c[..., :OFF_KVC], q_norm_g[l], w_uq[l], qk_norm_q[l])
            attn_c = _attention(q_c, k_ctx, v_ctx)
            sgu_c = _sgu(proj_c[..., OFF_U:OFF_V], proj_c[..., OFF_V:OFF_GATE], sgu_norm_g[l],
                         sgu_norm_b[l], w_spatial[l], b_spatial[l])
            ctx = ctx + g1c * _merge(attn_c, sgu_c, proj_c[..., OFF_GATE:], w_br_attn[l],
                                     w_br_sgu[l], w_out[l])
            ctx = ctx + g2c * _swiglu(_modulate(_rms(ctx, norm2_g[l]), sh2c, sc2c),
                                      w_ffn_in[l], w_ffn_out[l])
        x = x_new
    return x
```

```python
import functools
import math

import jax
import jax.numpy as jnp
import numpy as np
from jax import lax
from jax.experimental import pallas as pl
from jax.experimental.pallas import tpu as pltpu

F32 = jnp.float32
BF16 = jnp.bfloat16

GRID_W = 64
MLA_HEADS = 8
NOPE_DIM = 128
ROPE_DIM = 64
V_DIM = 128
QK_DIM = NOPE_DIM + ROPE_DIM
ROPE_THETA = 10000.0
CHUNK = 128
SGU_GROUPS = 8
N_MOD = 6
EPS = 1e-6

LANES = 128
HEAD_PAD = 2 * LANES
VMEM_LIMIT_BYTES = 56 * 1024 * 1024

_Q = ROPE_DIM // 4
ROPE_PERM = np.concatenate([np.arange(0, _Q), np.arange(2 * _Q, 3 * _Q),
                            np.arange(_Q, 2 * _Q), np.arange(3 * _Q, 4 * _Q)])


def _params(*semantics):
    return pltpu.CompilerParams(dimension_semantics=semantics,
                                vmem_limit_bytes=VMEM_LIMIT_BYTES)


def _dot(a, b):
    return jnp.dot(a, b, preferred_element_type=F32)


def _dot_nt(a, b):
    return lax.dot_general(a, b, (((1,), (1,)), ((), ())), preferred_element_type=F32)


def _mod_kernel(cc_ref, w_ref, b_ref, o_ref):
    cc = cc_ref[...]
    s = cc * jax.nn.sigmoid(cc)
    o_ref[...] = _dot(s.astype(BF16), w_ref[...].astype(BF16)) + b_ref[...]


def _mod(cc, w_mod, b_mod, tn=1024):
    rows, d = cc.shape
    n_out = w_mod.shape[1]
    return pl.pallas_call(
        _mod_kernel,
        grid=(n_out // tn,),
        in_specs=[pl.BlockSpec((rows, d), lambda j: (0, 0)),
                  pl.BlockSpec((d, tn), lambda j: (0, j)),
                  pl.BlockSpec((1, tn), lambda j: (0, j))],
        out_specs=pl.BlockSpec((rows, tn), lambda j: (0, j)),
        out_shape=jax.ShapeDtypeStruct((rows, n_out), F32),
        compiler_params=_params("arbitrary"),
        name="mod",
    )(cc, w_mod, b_mod.reshape(1, n_out))


def _rope_table_kernel(ang_ref, c_ref, s1_ref, s2_ref):
    ang = ang_ref[...]
    lane = lax.broadcasted_iota(jnp.int32, ang.shape, 1)
    cos = jnp.cos(ang)
    sin = jnp.sin(ang)
    c_ref[...] = jnp.where(lane < ROPE_DIM, cos, 0.0)
    s1_ref[...] = jnp.where(lane < ROPE_DIM // 2, -sin, 0.0)
    s2_ref[...] = jnp.where((lane >= ROPE_DIM // 2) & (lane < ROPE_DIM), sin, 0.0)


def _rope_tables(n):
    pos = np.arange(n)
    freqs = ROPE_THETA ** (-np.arange(_Q, dtype=np.float32) / _Q)
    ang_r = (pos // GRID_W).astype(np.float32)[:, None] * freqs[None, :]
    ang_c = (pos % GRID_W).astype(np.float32)[:, None] * freqs[None, :]
    ang = np.concatenate([ang_r, ang_c, ang_r, ang_c,
                          np.zeros((n, LANES - ROPE_DIM), np.float32)], axis=1)
    tm = 512
    spec = pl.BlockSpec((tm, LANES), lambda i: (i, 0))
    shape = jax.ShapeDtypeStruct((n, LANES), F32)
    return pl.pallas_call(
        _rope_table_kernel, grid=(n // tm,), in_specs=[spec], out_specs=[spec] * 3,
        out_shape=[shape] * 3, compiler_params=_params("arbitrary"), name="rope_tables",
    )(jnp.asarray(ang))


def _hnorm_kernel(x_ref, g_ref, sh_ref, sc_ref, o_ref):
    x = x_ref[...]
    ms = jnp.mean(x * x, axis=-1, keepdims=True)
    y = x * lax.rsqrt(ms + EPS) * g_ref[...]
    o_ref[...] = (y * (1.0 + sc_ref[...]) + sh_ref[...]).astype(o_ref.dtype)


def _hnorm(x2d, g, sh, sc, rows_per_mod, tm=512):
    t, d = x2d.shape
    tpb = rows_per_mod // tm
    mod_spec = pl.BlockSpec((None, 1, d), lambda i: (i // tpb, 0, 0))
    return pl.pallas_call(
        _hnorm_kernel,
        grid=(t // tm,),
        in_specs=[pl.BlockSpec((tm, d), lambda i: (i, 0)),
                  pl.BlockSpec((1, d), lambda i: (0, 0)),
                  mod_spec, mod_spec],
        out_specs=pl.BlockSpec((tm, d), lambda i: (i, 0)),
        out_shape=jax.ShapeDtypeStruct((t, d), BF16),
        compiler_params=_params("arbitrary"),
        name="hnorm",
    )(x2d, g, sh, sc)


def _rms_rows(x, g):
    ms = jnp.mean(x * x, axis=-1, keepdims=True)
    return x * lax.rsqrt(ms + EPS) * g


def _qk_head(xh, g, c, s1, s2, out_scale):
    ss = jnp.sum(xh * xh, axis=-1, keepdims=True)
    y = xh * lax.rsqrt(ss * (1.0 / QK_DIM) + EPS) * g
    nope = y[:, :NOPE_DIM]
    r = y[:, NOPE_DIM:]
    rot = (r * c + pltpu.roll(r, LANES - ROPE_DIM // 2, 1) * s1
           + pltpu.roll(r, ROPE_DIM // 2, 1) * s2)
    return nope * out_scale, rot * out_scale


def _qkv_kernel(h_ref, wqkv_ref, gq_ref, gkv_ref, wuq_ref, wukv_ref, gqk_q_ref, gqk_k_ref,
                c_ref, s1_ref, s2_ref, *out_refs, q_lora, kv_lora, with_q):
    acc = _dot(h_ref[...], wqkv_ref[...])
    c, s1, s2 = c_ref[...], s1_ref[...], s2_ref[...]
    if with_q:
        q_ref, k_ref, v_ref = out_refs
        qn = _rms_rows(acc[:, :q_lora], gq_ref[...]).astype(BF16)
        q_all = _dot(qn, wuq_ref[...])
        gq = gqk_q_ref[...]
        for hd in range(MLA_HEADS):
            nope, rot = _qk_head(q_all[:, hd * HEAD_PAD:(hd + 1) * HEAD_PAD], gq, c, s1, s2,
                                 QK_DIM ** -0.5)
            q_ref[0, hd, :, :NOPE_DIM] = nope.astype(BF16)
            q_ref[0, hd, :, NOPE_DIM:] = rot.astype(BF16)
    else:
        k_ref, v_ref = out_refs
    kvn = _rms_rows(acc[:, q_lora:q_lora + kv_lora], gkv_ref[...]).astype(BF16)
    kv_all = _dot(kvn, wukv_ref[...])
    k_rope = acc[:, q_lora + kv_lora:]
    gk = gqk_k_ref[...]
    for hd in range(MLA_HEADS):
        base = hd * (NOPE_DIM + V_DIM)
        xh = jnp.concatenate([kv_all[:, base:base + NOPE_DIM], k_rope], axis=-1)
        nope, rot = _qk_head(xh, gk, c, s1, s2, 1.0)
        k_ref[0, hd, :, :NOPE_DIM] = nope.astype(BF16)
        k_ref[0, hd, :, NOPE_DIM:] = rot.astype(BF16)
        v_ref[0, hd] = kv_all[:, base + NOPE_DIM:base + NOPE_DIM + V_DIM].astype(BF16)


def _qkv(h2d, batch, wqkv, gq, gkv, wuq, wukv, gqk_q, gqk_k, tabs, with_q, tm=512):
    t, d = h2d.shape
    n = t // batch
    tpb = n // tm
    q_lora, kv_lora = gq.shape[1], gkv.shape[1]
    const = lambda a: pl.BlockSpec(a.shape, lambda i: (0,) * a.ndim)
    tab_spec = pl.BlockSpec((tm, LANES), lambda i: (i % tpb, 0))
    qk_spec = pl.BlockSpec((1, MLA_HEADS, tm, HEAD_PAD), lambda i: (i // tpb, 0, i % tpb, 0))
    v_spec = pl.BlockSpec((1, MLA_HEADS, tm, V_DIM), lambda i: (i // tpb, 0, i % tpb, 0))
    qk_shape = jax.ShapeDtypeStruct((batch, MLA_HEADS, n, HEAD_PAD), BF16)
    v_shape = jax.ShapeDtypeStruct((batch, MLA_HEADS, n, V_DIM), BF16)
    out_specs = [qk_spec, qk_spec, v_spec] if with_q else [qk_spec, v_spec]
    out_shape = [qk_shape, qk_shape, v_shape] if with_q else [qk_shape, v_shape]
    return pl.pallas_call(
        functools.partial(_qkv_kernel, q_lora=q_lora, kv_lora=kv_lora, with_q=with_q),
        grid=(t // tm,),
        in_specs=[pl.BlockSpec((tm, d), lambda i: (i, 0)),
                  const(wqkv), const(gq), const(gkv), const(wuq), const(wukv),
                  const(gqk_q), const(gqk_k), tab_spec, tab_spec, tab_spec],
        out_specs=out_specs, out_shape=out_shape,
        compiler_params=_params("arbitrary"),
        name="qkv_latent" if with_q else "kv_context",
    )(h2d, wqkv, gq, gkv, wuq, wukv, gqk_q, gqk_k, *tabs)


def _attn_kernel(q_ref, kl_ref, kc_ref, vl_ref, vc_ref, o_ref):
    q = q_ref[0, 0]
    s_l = _dot_nt(q, kl_ref[0, 0])
    s_c = _dot_nt(q, kc_ref[0, 0])
    m = jnp.maximum(jnp.max(s_l, axis=-1, keepdims=True), jnp.max(s_c, axis=-1, keepdims=True))
    p_l = jnp.exp(s_l - m)
    p_c = jnp.exp(s_c - m)
    denom = jnp.sum(p_l, axis=-1, keepdims=True) + jnp.sum(p_c, axis=-1, keepdims=True)
    o = _dot(p_l.astype(BF16), vl_ref[0, 0]) + _dot(p_c.astype(BF16), vc_ref[0, 0])
    o_ref[...] = (o / denom).astype(o_ref.dtype)


def _attention(q, k_lat, k_ctx, v_lat, v_ctx, tq=256):
    b, hds, n, _ = q.shape
    m_ctx = k_ctx.shape[2]
    nq = n // tq
    kv_spec = lambda rows, w: pl.BlockSpec((1, 1, rows, w), lambda bi, hi, i: (bi, hi, 0, 0))
    return pl.pallas_call(
        _attn_kernel,
        grid=(b, hds, nq),
        in_specs=[pl.BlockSpec((1, 1, tq, HEAD_PAD), lambda bi, hi, i: (bi, hi, i, 0)),
                  kv_spec(n, HEAD_PAD), kv_spec(m_ctx, HEAD_PAD),
                  kv_spec(n, V_DIM), kv_spec(m_ctx, V_DIM)],
        out_specs=pl.BlockSpec((tq, V_DIM), lambda bi, hi, i: (bi * nq + i, hi)),
        out_shape=jax.ShapeDtypeStruct((b * n, hds * V_DIM), BF16),
        compiler_params=_params("arbitrary", "arbitrary", "arbitrary"),
        name="attention",
    )(q, k_lat, k_ctx, v_lat, v_ctx)


def _gelu(x):
    return 0.5 * x * (1.0 + lax.erf(x * (2.0 ** -0.5)))


def _sgu_kernel(h_ref, w_ref, g_ref, b_ref, ws_ref, bs_ref, o_ref, *, width):
    acc = _dot(h_ref[...], w_ref[...])
    u = _gelu(acc[:, :width])
    v = _gelu(acc[:, width:])
    mu = jnp.mean(v, axis=-1, keepdims=True)
    vc = v - mu
    var = jnp.mean(vc * vc, axis=-1, keepdims=True)
    vn = (vc * lax.rsqrt(var + EPS) * g_ref[...] + b_ref[...]).astype(BF16)
    gd = width // SGU_GROUPS
    for ck in range(h_ref.shape[0] // CHUNK):
        rows = slice(ck * CHUNK, (ck + 1) * CHUNK)
        for g in range(SGU_GROUPS):
            cols = slice(g * gd, (g + 1) * gd)
            mixed = _dot(ws_ref[g], vn[rows, cols]) + bs_ref[g]
            o_ref[rows, cols] = (u[rows, cols] * mixed).astype(o_ref.dtype)


def _sgu(h2d, w_uv, norm_g, norm_b, w_s, b_s_full, tm=512):
    t, d = h2d.shape
    width = w_uv.shape[1] // 2
    const = lambda a: pl.BlockSpec(a.shape, lambda i: (0,) * a.ndim)
    return pl.pallas_call(
        functools.partial(_sgu_kernel, width=width),
        grid=(t // tm,),
        in_specs=[pl.BlockSpec((tm, d), lambda i: (i, 0)), const(w_uv), const(norm_g),
                  const(norm_b), const(w_s), const(b_s_full)],
        out_specs=pl.BlockSpec((tm, width), lambda i: (i, 0)),
        out_shape=jax.ShapeDtypeStruct((t, width), BF16),
        compiler_params=_params("arbitrary"),
        name="sgu",
    )(h2d, w_uv, norm_g, norm_b, w_s, b_s_full)


def _merge_kernel(h_ref, a_ref, s_ref, wga_ref, wgs_ref, wa_ref, ws_ref, o_ref):
    h = h_ref[...]
    ga = jax.nn.sigmoid(_dot(h, wga_ref[...]))
    gs = jax.nn.sigmoid(_dot(h, wgs_ref[...]))
    merged = ga * _dot(a_ref[...], wa_ref[...]) + gs * _dot(s_ref[...], ws_ref[...])
    o_ref[...] = merged.astype(o_ref.dtype)


def _merge(h2d, attn_o, sgu_o, w_gate, w_br_attn, w_br_sgu, tm=512, tn=512):
    t, d = h2d.shape
    d_out = w_br_attn.shape[1]
    nj = d_out // tn
    row = lambda w: pl.BlockSpec((tm, w), lambda i, j: (i, 0))
    return pl.pallas_call(
        _merge_kernel,
        grid=(t // tm, nj),
        in_specs=[row(d), row(attn_o.shape[1]), row(sgu_o.shape[1]),
                  pl.BlockSpec((d, tn), lambda i, j: (0, j)),
                  pl.BlockSpec((d, tn), lambda i, j: (0, j + nj)),
                  pl.BlockSpec((w_br_attn.shape[0], tn), lambda i, j: (0, j)),
                  pl.BlockSpec((w_br_sgu.shape[0], tn), lambda i, j: (0, j))],
        out_specs=pl.BlockSpec((tm, tn), lambda i, j: (i, j)),
        out_shape=jax.ShapeDtypeStruct((t, d_out), BF16),
        compiler_params=_params("arbitrary", "arbitrary"),
        name="merge",
    )(h2d, attn_o, sgu_o, w_gate, w_gate, w_br_attn, w_br_sgu)


def _out_kernel(m_ref, w_ref, x_ref, g1_ref, g_ref, sh_ref, sc_ref, xn_ref, h2_ref):
    xn = x_ref[...] + g1_ref[...] * _dot(m_ref[...], w_ref[...])
    xn_ref[...] = xn
    h2_ref[...] = (_rms_rows(xn, g_ref[...]) * (1.0 + sc_ref[...]) + sh_ref[...]).astype(BF16)


def _out_proj(merged, w_out, x2d, g1, norm2_g, sh2, sc2, rows_per_mod, tm=256):
    t, d = x2d.shape
    tpb = rows_per_mod // tm
    mod_spec = pl.BlockSpec((None, 1, d), lambda i: (i // tpb, 0, 0))
    row = pl.BlockSpec((tm, d), lambda i: (i, 0))
    return pl.pallas_call(
        _out_kernel,
        grid=(t // tm,),
        in_specs=[row, pl.BlockSpec(w_out.shape, lambda i: (0, 0)), row, mod_spec,
                  pl.BlockSpec((1, d), lambda i: (0, 0)), mod_spec, mod_spec],
        out_specs=[row, row],
        out_shape=[jax.ShapeDtypeStruct((t, d), F32), jax.ShapeDtypeStruct((t, d), BF16)],
        compiler_params=_params("arbitrary"),
        name="out_proj",
    )(merged, w_out, x2d, g1, norm2_g, sh2, sc2)


def _ffn_in_kernel(h_ref, wa_ref, wb_ref, o_ref):
    h = h_ref[...]
    a = _dot(h, wa_ref[...])
    b = _dot(h, wb_ref[...])
    o_ref[...] = (a * jax.nn.sigmoid(a) * b).astype(o_ref.dtype)


def _ffn_in(h2, w_ffn_in, tm=1024, tn=512):
    t, d = h2.shape
    d_ff = w_ffn_in.shape[1] // 2
    nj = d_ff // tn
    return pl.pallas_call(
        _ffn_in_kernel,
        grid=(t // tm, nj),
        in_specs=[pl.BlockSpec((tm, d), lambda i, j: (i, 0)),
                  pl.BlockSpec((d, tn), lambda i, j: (0, j)),
                  pl.BlockSpec((d, tn), lambda i, j: (0, j + nj))],
        out_specs=pl.BlockSpec((tm, tn), lambda i, j: (i, j)),
        out_shape=jax.ShapeDtypeStruct((t, d_ff), BF16),
        compiler_params=_params("arbitrary", "arbitrary"),
        name="ffn_in",
    )(h2, w_ffn_in, w_ffn_in)


def _ffn_out_kernel(a_ref, w_ref, x_ref, g2_ref, o_ref):
    o_ref[...] = x_ref[...] + g2_ref[...] * _dot(a_ref[...], w_ref[...])


def _ffn_out(act, w_ffn_out, x_new, g2, rows_per_mod, tm=512, tn=512):
    t, d_ff = act.shape
    d = w_ffn_out.shape[1]
    tpb = rows_per_mod // tm
    return pl.pallas_call(
        _ffn_out_kernel,
        grid=(t // tm, d // tn),
        in_specs=[pl.BlockSpec((tm, d_ff), lambda i, j: (i, 0)),
                  pl.BlockSpec((d_ff, tn), lambda i, j: (0, j)),
                  pl.BlockSpec((tm, tn), lambda i, j: (i, j)),
                  pl.BlockSpec((None, 1, tn), lambda i, j: (i // tpb, 0, j))],
        out_specs=pl.BlockSpec((tm, tn), lambda i, j: (i, j)),
        out_shape=jax.ShapeDtypeStruct((t, d), F32),
        compiler_params=_params("arbitrary", "arbitrary"),
        name="ffn_out",
    )(act, w_ffn_out, x_new, g2)


def _pad_head_vec(g):
    return jnp.concatenate([g[:NOPE_DIM], g[NOPE_DIM:][ROPE_PERM],
                            jnp.zeros((HEAD_PAD - QK_DIM,), g.dtype)])[None, :]


def kernel(x, c, ctx, c_ctx, w_mod, b_mod, norm1_g, w_in, q_norm_g, kv_norm_g, w_uq, w_ukv,
           qk_norm_q, qk_norm_k, sgu_norm_g, sgu_norm_b, w_spatial, b_spatial, w_br_attn,
           w_br_sgu, w_out, norm2_g, w_ffn_in, w_ffn_out):
    batch, n, d = x.shape
    m_ctx = ctx.shape[1]
    depth = w_mod.shape[0]
    q_lora, kv_lora = q_norm_g.shape[1], kv_norm_g.shape[1]
    width = sgu_norm_g.shape[1]
    off_u = q_lora + kv_lora + ROPE_DIM
    off_gate = off_u + 2 * width
    assert depth == 1 and n % GRID_W == 0

    tabs = _rope_tables(n)
    zeros = jnp.zeros((m_ctx, LANES), F32)
    tabs_ctx = (jnp.where(jnp.arange(LANES) < ROPE_DIM, 1.0, 0.0).astype(F32)[None, :] + zeros,
                zeros, zeros)

    mod_rows = 8
    cc = jnp.concatenate([c, c_ctx[None, :], jnp.zeros((mod_rows - batch - 1, d), F32)], axis=0)
    x2d = x.reshape(batch * n, d)
    ctx2d = ctx.reshape(batch * m_ctx, d)

    l = 0
    w_in_l = w_in[l]
    w_qkv = jnp.concatenate(
        [w_in_l[:, :q_lora + kv_lora], w_in_l[:, q_lora + kv_lora:off_u][:, ROPE_PERM],
         jnp.zeros((d, LANES - ROPE_DIM), F32)], axis=1).astype(BF16)
    w_uv = w_in_l[:, off_u:off_gate].astype(BF16)
    w_gate = w_in_l[:, off_gate:].astype(BF16)
    wq3 = w_uq[l].reshape(q_lora, MLA_HEADS, QK_DIM)
    w_uq_p = jnp.concatenate(
        [wq3[:, :, :NOPE_DIM], wq3[:, :, NOPE_DIM:][:, :, ROPE_PERM],
         jnp.zeros((q_lora, MLA_HEADS, HEAD_PAD - QK_DIM), F32)], axis=2
    ).reshape(q_lora, MLA_HEADS * HEAD_PAD).astype(BF16)
    w_ukv_b = w_ukv[l].astype(BF16)
    gqk_q = _pad_head_vec(qk_norm_q[l])
    gqk_k = _pad_head_vec(qk_norm_k[l])
    w_s = w_spatial[l].astype(BF16)
    b_s_full = jnp.broadcast_to(b_spatial[l][:, :, None], (SGU_GROUPS, CHUNK, width // SGU_GROUPS))

    mod = _mod(cc, w_mod[l], b_mod[l])
    mod_lat = mod[:batch].reshape(batch, 1, N_MOD, d)
    sh1, sc1, g1, sh2, sc2, g2 = [mod_lat[:, :, k, :] for k in range(N_MOD)]
    sh1c = mod[batch:batch + 1, 0:d][None]
    sc1c = mod[batch:batch + 1, d:2 * d][None]

    h = _hnorm(x2d, norm1_g[l][None, :], sh1, sc1, n)
    ctx_h = _hnorm(ctx2d, norm1_g[l][None, :], sh1c, sc1c, batch * m_ctx, tm=m_ctx)

    q, k_lat, v_lat = _qkv(h, batch, w_qkv, q_norm_g[l][None, :], kv_norm_g[l][None, :], w_uq_p,
                           w_ukv_b, gqk_q, gqk_k, tabs, with_q=True)
    k_ctx, v_ctx = _qkv(ctx_h, batch, w_qkv, q_norm_g[l][None, :], kv_norm_g[l][None, :], w_uq_p,
                        w_ukv_b, gqk_q, gqk_k, tabs_ctx, with_q=False, tm=m_ctx)
    attn_o = _attention(q, k_lat, k_ctx, v_lat, v_ctx)
    sgu_o = _sgu(h, w_uv, sgu_norm_g[l][None, :], sgu_norm_b[l][None, :], w_s, b_s_full)
    merged = _merge(h, attn_o, sgu_o, w_gate, w_br_attn[l].astype(BF16), w_br_sgu[l].astype(BF16))
    x_new, h2 = _out_proj(merged, w_out[l].astype(BF16), x2d, g1, norm2_g[l][None, :], sh2, sc2, n)
    act = _ffn_in(h2, w_ffn_in[l].astype(BF16))
    y = _ffn_out(act, w_ffn_out[l].astype(BF16), x_new, g2, n)
    return y.reshape(batch, n, d)
```

```python
import functools
import math

import jax
import jax.numpy as jnp
import numpy as np
from jax import lax
from jax.experimental import pallas as pl
from jax.experimental.pallas import tpu as pltpu

F32 = jnp.float32
BF16 = jnp.bfloat16

GRID_W = 64
MLA_HEADS = 8
NOPE_DIM = 128
ROPE_DIM = 64
V_DIM = 128
QK_DIM = NOPE_DIM + ROPE_DIM
ROPE_THETA = 10000.0
CHUNK = 128
SGU_GROUPS = 8
N_MOD = 6
EPS = 1e-6

LANES = 128
HEAD_PAD = 2 * LANES
VMEM_LIMIT_BYTES = 56 * 1024 * 1024

_Q = ROPE_DIM // 4
ROPE_PERM = np.concatenate([np.arange(0, _Q), np.arange(2 * _Q, 3 * _Q),
                            np.arange(_Q, 2 * _Q), np.arange(3 * _Q, 4 * _Q)])


def _params(*semantics):
    return pltpu.CompilerParams(dimension_semantics=semantics,
                                vmem_limit_bytes=VMEM_LIMIT_BYTES)


def _dot(a, b):
    return jnp.dot(a, b, preferred_element_type=F32)


def _dot_nt(a, b):
    return lax.dot_general(a, b, (((1,), (1,)), ((), ())), preferred_element_type=F32)


def _mod_kernel(cc_ref, w_ref, b_ref, o_ref):
    cc = cc_ref[...]
    s = cc * jax.nn.sigmoid(cc)
    o_ref[...] = _dot(s.astype(BF16), w_ref[...].astype(BF16)) + b_ref[...]


def _mod(cc, w_mod, b_mod, tn=1024):
    rows, d = cc.shape
    n_out = w_mod.shape[1]
    return pl.pallas_call(
        _mod_kernel,
        grid=(n_out // tn,),
        in_specs=[pl.BlockSpec((rows, d), lambda j: (0, 0)),
                  pl.BlockSpec((d, tn), lambda j: (0, j)),
                  pl.BlockSpec((1, tn), lambda j: (0, j))],
        out_specs=pl.BlockSpec((rows, tn), lambda j: (0, j)),
        out_shape=jax.ShapeDtypeStruct((rows, n_out), F32),
        compiler_params=_params("arbitrary"),
        name="mod",
    )(cc, w_mod, b_mod.reshape(1, n_out))


def _rope_table_kernel(ang_ref, c_ref, s1_ref, s2_ref):
    ang = ang_ref[...]
    lane = lax.broadcasted_iota(jnp.int32, ang.shape, 1)
    cos = jnp.cos(ang)
    sin = jnp.sin(ang)
    c_ref[...] = jnp.where(lane < ROPE_DIM, cos, 0.0)
    s1_ref[...] = jnp.where(lane < ROPE_DIM // 2, -sin, 0.0)
    s2_ref[...] = jnp.where((lane >= ROPE_DIM // 2) & (lane < ROPE_DIM), sin, 0.0)


def _rope_tables(n):
    pos = np.arange(n)
    freqs = ROPE_THETA ** (-np.arange(_Q, dtype=np.float32) / _Q)
    ang_r = (pos // GRID_W).astype(np.float32)[:, None] * freqs[None, :]
    ang_c = (pos % GRID_W).astype(np.float32)[:, None] * freqs[None, :]
    ang = np.concatenate([ang_r, ang_c, ang_r, ang_c,
                          np.zeros((n, LANES - ROPE_DIM), np.float32)], axis=1)
    tm = 512
    spec = pl.BlockSpec((tm, LANES), lambda i: (i, 0))
    shape = jax.ShapeDtypeStruct((n, LANES), F32)
    return pl.pallas_call(
        _rope_table_kernel, grid=(n // tm,), in_specs=[spec], out_specs=[spec] * 3,
        out_shape=[shape] * 3, compiler_params=_params("arbitrary"), name="rope_tables",
    )(jnp.asarray(ang))


def _hnorm_kernel(x_ref, g_ref, sh_ref, sc_ref, o_ref):
    x = x_ref[...]
    ms = jnp.mean(x * x, axis=-1, keepdims=True)
    y = x * lax.rsqrt(ms + EPS) * g_ref[...]
    o_ref[...] = (y * (1.0 + sc_ref[...]) + sh_ref[...]).astype(o_ref.dtype)


def _hnorm(x2d, g, sh, sc, rows_per_mod, tm=512):
    t, d = x2d.shape
    tpb = rows_per_mod // tm
    mod_spec = pl.BlockSpec((None, 1, d), lambda i: (i // tpb, 0, 0))
    return pl.pallas_call(
        _hnorm_kernel,
        grid=(t // tm,),
        in_specs=[pl.BlockSpec((tm, d), lambda i: (i, 0)),
                  pl.BlockSpec((1, d), lambda i: (0, 0)),
                  mod_spec, mod_spec],
        out_specs=pl.BlockSpec((tm, d), lambda i: (i, 0)),
        out_shape=jax.ShapeDtypeStruct((t, d), BF16),
        compiler_params=_params("arbitrary"),
        name="hnorm",
    )(x2d, g, sh, sc)


def _rms_rows(x, g):
    ms = jnp.mean(x * x, axis=-1, keepdims=True)
    return x * lax.rsqrt(ms + EPS) * g


def _qk_head(xh, g, c, s1, s2, out_scale):
    ss = jnp.sum(xh * xh, axis=-1, keepdims=True)
    y = xh * lax.rsqrt(ss * (1.0 / QK_DIM) + EPS) * g
    nope = y[:, :NOPE_DIM]
    r = y[:, NOPE_DIM:]
    rot = (r * c + pltpu.roll(r, LANES - ROPE_DIM // 2, 1) * s1
           + pltpu.roll(r, ROPE_DIM // 2, 1) * s2)
    return nope * out_scale, rot * out_scale


def _qkv_kernel(h_ref, wqkv_ref, gq_ref, gkv_ref, wuq_ref, wukv_ref, gqk_q_ref, gqk_k_ref,
                c_ref, s1_ref, s2_ref, *out_refs, q_lora, kv_lora, with_q):
    acc = _dot(h_ref[...], wqkv_ref[...])
    c, s1, s2 = c_ref[...], s1_ref[...], s2_ref[...]
    if with_q:
        q_ref, k_ref, v_ref = out_refs
        qn = _rms_rows(acc[:, :q_lora], gq_ref[...]).astype(BF16)
        q_all = _dot(qn, wuq_ref[...])
        gq = gqk_q_ref[...]
        for hd in range(MLA_HEADS):
            nope, rot = _qk_head(q_all[:, hd * HEAD_PAD:(hd + 1) * HEAD_PAD], gq, c, s1, s2,
                                 QK_DIM ** -0.5 * math.log2(math.e))
            q_ref[0, hd, :, :NOPE_DIM] = nope.astype(BF16)
            q_ref[0, hd, :, NOPE_DIM:] = rot.astype(BF16)
    else:
        k_ref, v_ref = out_refs
    kvn = _rms_rows(acc[:, q_lora:q_lora + kv_lora], gkv_ref[...]).astype(BF16)
    kv_all = _dot(kvn, wukv_ref[...])
    k_rope = acc[:, q_lora + kv_lora:]
    gk = gqk_k_ref[...]
    lane = lax.broadcasted_iota(jnp.int32, (acc.shape[0], LANES), 1)
    ones_col = jnp.where(lane == 0, 1.0, 0.0).astype(BF16)
    for hd in range(MLA_HEADS):
        base = hd * (NOPE_DIM + V_DIM)
        xh = jnp.concatenate([kv_all[:, base:base + NOPE_DIM], k_rope], axis=-1)
        nope, rot = _qk_head(xh, gk, c, s1, s2, 1.0)
        k_ref[0, hd, :, :NOPE_DIM] = nope.astype(BF16)
        k_ref[0, hd, :, NOPE_DIM:] = rot.astype(BF16)
        v_ref[0, hd, :, :V_DIM] = kv_all[:, base + NOPE_DIM:base + NOPE_DIM + V_DIM].astype(BF16)
        v_ref[0, hd, :, V_DIM:] = ones_col


def _qkv(h2d, batch, wqkv, gq, gkv, wuq, wukv, gqk_q, gqk_k, tabs, with_q, tm=512):
    t, d = h2d.shape
    n = t // batch
    tpb = n // tm
    q_lora, kv_lora = gq.shape[1], gkv.shape[1]
    const = lambda a: pl.BlockSpec(a.shape, lambda i: (0,) * a.ndim)
    tab_spec = pl.BlockSpec((tm, LANES), lambda i: (i % tpb, 0))
    qk_spec = pl.BlockSpec((1, MLA_HEADS, tm, HEAD_PAD), lambda i: (i // tpb, 0, i % tpb, 0))
    qk_shape = jax.ShapeDtypeStruct((batch, MLA_HEADS, n, HEAD_PAD), BF16)
    out_specs = [qk_spec] * (3 if with_q else 2)
    out_shape = [qk_shape] * (3 if with_q else 2)
    return pl.pallas_call(
        functools.partial(_qkv_kernel, q_lora=q_lora, kv_lora=kv_lora, with_q=with_q),
        grid=(t // tm,),
        in_specs=[pl.BlockSpec((tm, d), lambda i: (i, 0)),
                  const(wqkv), const(gq), const(gkv), const(wuq), const(wukv),
                  const(gqk_q), const(gqk_k), tab_spec, tab_spec, tab_spec],
        out_specs=out_specs, out_shape=out_shape,
        compiler_params=_params("arbitrary"),
        name="qkv_latent" if with_q else "kv_context",
    )(h2d, wqkv, gq, gkv, wuq, wukv, gqk_q, gqk_k, *tabs)


def _attn_kernel(q_ref, kl_ref, kc_ref, vl_ref, vc_ref, o_ref, s_ref, *, sub, kchunk):
    n, m_ctx = kl_ref.shape[2], kc_ref.shape[2]
    chunks = [(kl_ref, vl_ref, j * kchunk, kchunk, j * kchunk) for j in range(n // kchunk)]
    chunks.append((kc_ref, vc_ref, 0, m_ctx, n))
    n_sub = q_ref.shape[2] // sub
    m_prev = None
    for t in range(n_sub + 1):
        slot = t % 2
        if t < n_sub:
            q = q_ref[0, 0, t * sub:(t + 1) * sub, :]
        m_vec = None
        acc = None
        for k_r, v_r, row0, size, col0 in chunks:
            if t < n_sub:
                s = _dot_nt(q, k_r[0, 0, row0:row0 + size, :])
                s_ref[slot, :, col0:col0 + size] = s
                for j in range(size // LANES):
                    tile = s[:, j * LANES:(j + 1) * LANES]
                    m_vec = tile if m_vec is None else jnp.maximum(m_vec, tile)
            if t >= 1:
                p = jnp.exp2(s_ref[1 - slot, :, col0:col0 + size] - m_prev).astype(BF16)
                d = _dot(p, v_r[0, 0, row0:row0 + size, :])
                acc = d if acc is None else acc + d
        if t >= 1:
            o = acc[:, :V_DIM] / acc[:, V_DIM:V_DIM + 1]
            o_ref[(t - 1) * sub:t * sub, :] = o.astype(o_ref.dtype)
        if t < n_sub:
            m_prev = jnp.max(m_vec, axis=-1, keepdims=True)


def _attention(q, k_lat, k_ctx, v_lat, v_ctx, tq=1024, sub=256, kchunk=512):
    b, hds, n, _ = q.shape
    m_ctx = k_ctx.shape[2]
    nq = n // tq
    kv_spec = lambda rows: pl.BlockSpec((1, 1, rows, HEAD_PAD), lambda bi, hi, i: (bi, hi, 0, 0))
    return pl.pallas_call(
        functools.partial(_attn_kernel, sub=sub, kchunk=kchunk),
        grid=(b, hds, nq),
        in_specs=[pl.BlockSpec((1, 1, tq, HEAD_PAD), lambda bi, hi, i: (bi, hi, i, 0)),
                  kv_spec(n), kv_spec(m_ctx), kv_spec(n), kv_spec(m_ctx)],
        out_specs=pl.BlockSpec((tq, V_DIM), lambda bi, hi, i: (bi * nq + i, hi)),
        out_shape=jax.ShapeDtypeStruct((b * n, hds * V_DIM), BF16),
        scratch_shapes=[pltpu.VMEM((2, sub, n + m_ctx), F32)],
        compiler_params=_params("arbitrary", "arbitrary", "arbitrary"),
        name="attention",
    )(q, k_lat, k_ctx, v_lat, v_ctx)


def _gelu(x):
    return 0.5 * x * (1.0 + lax.erf(x * (2.0 ** -0.5)))


def _sgu_kernel(h_ref, w_ref, g_ref, b_ref, ws_ref, bs_ref, o_ref, *, width):
    acc = _dot(h_ref[...], w_ref[...])
    u = _gelu(acc[:, :width])
    v = _gelu(acc[:, width:])
    mu = jnp.mean(v, axis=-1, keepdims=True)
    vc = v - mu
    var = jnp.mean(vc * vc, axis=-1, keepdims=True)
    vn = (vc * lax.rsqrt(var + EPS) * g_ref[...] + b_ref[...]).astype(BF16)
    gd = width // SGU_GROUPS
    for ck in range(h_ref.shape[0] // CHUNK):
        rows = slice(ck * CHUNK, (ck + 1) * CHUNK)
        for g in range(SGU_GROUPS):
            cols = slice(g * gd, (g + 1) * gd)
            mixed = _dot(ws_ref[g], vn[rows, cols]) + bs_ref[g]
            o_ref[rows, cols] = (u[rows, cols] * mixed).astype(o_ref.dtype)


def _sgu(h2d, w_uv, norm_g, norm_b, w_s, b_s_full, tm=512):
    t, d = h2d.shape
    width = w_uv.shape[1] // 2
    const = lambda a: pl.BlockSpec(a.shape, lambda i: (0,) * a.ndim)
    return pl.pallas_call(
        functools.partial(_sgu_kernel, width=width),
        grid=(t // tm,),
        in_specs=[pl.BlockSpec((tm, d), lambda i: (i, 0)), const(w_uv), const(norm_g),
                  const(norm_b), const(w_s), const(b_s_full)],
        out_specs=pl.BlockSpec((tm, width), lambda i: (i, 0)),
        out_shape=jax.ShapeDtypeStruct((t, width), BF16),
        compiler_params=_params("arbitrary"),
        name="sgu",
    )(h2d, w_uv, norm_g, norm_b, w_s, b_s_full)


def _merge_kernel(h_ref, a_ref, s_ref, wga_ref, wgs_ref, wa_ref, ws_ref, o_ref):
    h = h_ref[...]
    ga = jax.nn.sigmoid(_dot(h, wga_ref[...]))
    gs = jax.nn.sigmoid(_dot(h, wgs_ref[...]))
    merged = ga * _dot(a_ref[...], wa_ref[...]) + gs * _dot(s_ref[...], ws_ref[...])
    o_ref[...] = merged.astype(o_ref.dtype)


def _merge(h2d, attn_o, sgu_o, w_gate, w_br_attn, w_br_sgu, tm=512, tn=512):
    t, d = h2d.shape
    d_out = w_br_attn.shape[1]
    nj = d_out // tn
    row = lambda w: pl.BlockSpec((tm, w), lambda i, j: (i, 0))
    return pl.pallas_call(
        _merge_kernel,
        grid=(t // tm, nj),
        in_specs=[row(d), row(attn_o.shape[1]), row(sgu_o.shape[1]),
                  pl.BlockSpec((d, tn), lambda i, j: (0, j)),
                  pl.BlockSpec((d, tn), lambda i, j: (0, j + nj)),
                  pl.BlockSpec((w_br_attn.shape[0], tn), lambda i, j: (0, j)),
                  pl.BlockSpec((w_br_sgu.shape[0], tn), lambda i, j: (0, j))],
        out_specs=pl.BlockSpec((tm, tn), lambda i, j: (i, j)),
        out_shape=jax.ShapeDtypeStruct((t, d_out), BF16),
        compiler_params=_params("arbitrary", "arbitrary"),
        name="merge",
    )(h2d, attn_o, sgu_o, w_gate, w_gate, w_br_attn, w_br_sgu)


def _out_kernel(m_ref, w_ref, x_ref, g1_ref, g_ref, sh_ref, sc_ref, xn_ref, h2_ref):
    xn = x_ref[...] + g1_ref[...] * _dot(m_ref[...], w_ref[...])
    xn_ref[...] = xn
    h2_ref[...] = (_rms_rows(xn, g_ref[...]) * (1.0 + sc_ref[...]) + sh_ref[...]).astype(BF16)


def _out_proj(merged, w_out, x2d, g1, norm2_g, sh2, sc2, rows_per_mod, tm=256):
    t, d = x2d.shape
    tpb = rows_per_mod // tm
    mod_spec = pl.BlockSpec((None, 1, d), lambda i: (i // tpb, 0, 0))
    row = pl.BlockSpec((tm, d), lambda i: (i, 0))
    return pl.pallas_call(
        _out_kernel,
        grid=(t // tm,),
        in_specs=[row, pl.BlockSpec(w_out.shape, lambda i: (0, 0)), row, mod_spec,
                  pl.BlockSpec((1, d), lambda i: (0, 0)), mod_spec, mod_spec],
        out_specs=[row, row],
        out_shape=[jax.ShapeDtypeStruct((t, d), F32), jax.ShapeDtypeStruct((t, d), BF16)],
        compiler_params=_params("arbitrary"),
        name="out_proj",
    )(merged, w_out, x2d, g1, norm2_g, sh2, sc2)


def _ffn_in_kernel(h_ref, wa_ref, wb_ref, o_ref):
    h = h_ref[...]
    a = _dot(h, wa_ref[...])
    b = _dot(h, wb_ref[...])
    o_ref[...] = (a * jax.nn.sigmoid(a) * b).astype(o_ref.dtype)


def _ffn_in(h2, w_ffn_in, tm=1024, tn=512):
    t, d = h2.shape
    d_ff = w_ffn_in.shape[1] // 2
    nj = d_ff // tn
    return pl.pallas_call(
        _ffn_in_kernel,
        grid=(t // tm, nj),
        in_specs=[pl.BlockSpec((tm, d), lambda i, j: (i, 0)),
                  pl.BlockSpec((d, tn), lambda i, j: (0, j)),
                  pl.BlockSpec((d, tn), lambda i, j: (0, j + nj))],
        out_specs=pl.BlockSpec((tm, tn), lambda i, j: (i, j)),
        out_shape=jax.ShapeDtypeStruct((t, d_ff), BF16),
        compiler_params=_params("arbitrary", "arbitrary"),
        name="ffn_in",
    )(h2, w_ffn_in, w_ffn_in)


def _ffn_out_kernel(a_ref, w_ref, x_ref, g2_ref, o_ref):
    o_ref[...] = x_ref[...] + g2_ref[...] * _dot(a_ref[...], w_ref[...])


def _ffn_out(act, w_ffn_out, x_new, g2, rows_per_mod, tm=512, tn=512):
    t, d_ff = act.shape
    d = w_ffn_out.shape[1]
    tpb = rows_per_mod // tm
    return pl.pallas_call(
        _ffn_out_kernel,
        grid=(t // tm, d // tn),
        in_specs=[pl.BlockSpec((tm, d_ff), lambda i, j: (i, 0)),
                  pl.BlockSpec((d_ff, tn), lambda i, j: (0, j)),
                  pl.BlockSpec((tm, tn), lambda i, j: (i, j)),
                  pl.BlockSpec((None, 1, tn), lambda i, j: (i // tpb, 0, j))],
        out_specs=pl.BlockSpec((tm, tn), lambda i, j: (i, j)),
        out_shape=jax.ShapeDtypeStruct((t, d), F32),
        compiler_params=_params("arbitrary", "arbitrary"),
        name="ffn_out",
    )(act, w_ffn_out, x_new, g2)


def _pad_head_vec(g):
    return jnp.concatenate([g[:NOPE_DIM], g[NOPE_DIM:][ROPE_PERM],
                            jnp.zeros((HEAD_PAD - QK_DIM,), g.dtype)])[None, :]


def kernel(x, c, ctx, c_ctx, w_mod, b_mod, norm1_g, w_in, q_norm_g, kv_norm_g, w_uq, w_ukv,
           qk_norm_q, qk_norm_k, sgu_norm_g, sgu_norm_b, w_spatial, b_spatial, w_br_attn,
           w_br_sgu, w_out, norm2_g, w_ffn_in, w_ffn_out):
    batch, n, d = x.shape
    m_ctx = ctx.shape[1]
    depth = w_mod.shape[0]
    q_lora, kv_lora = q_norm_g.shape[1], kv_norm_g.shape[1]
    width = sgu_norm_g.shape[1]
    off_u = q_lora + kv_lora + ROPE_DIM
    off_gate = off_u + 2 * width
    assert depth == 1 and n % GRID_W == 0

    tabs = _rope_tables(n)
    zeros = jnp.zeros((m_ctx, LANES), F32)
    tabs_ctx = (jnp.where(jnp.arange(LANES) < ROPE_DIM, 1.0, 0.0).astype(F32)[None, :] + zeros,
                zeros, zeros)

    mod_rows = 8
    cc = jnp.concatenate([c, c_ctx[None, :], jnp.zeros((mod_rows - batch - 1, d), F32)], axis=0)
    x2d = x.reshape(batch * n, d)
    ctx2d = ctx.reshape(batch * m_ctx, d)

    l = 0
    w_in_l = w_in[l]
    w_qkv = jnp.concatenate(
        [w_in_l[:, :q_lora + kv_lora], w_in_l[:, q_lora + kv_lora:off_u][:, ROPE_PERM],
         jnp.zeros((d, LANES - ROPE_DIM), F32)], axis=1).astype(BF16)
    w_uv = w_in_l[:, off_u:off_gate].astype(BF16)
    w_gate = w_in_l[:, off_gate:].astype(BF16)
    wq3 = w_uq[l].reshape(q_lora, MLA_HEADS, QK_DIM)
    w_uq_p = jnp.concatenate(
        [wq3[:, :, :NOPE_DIM], wq3[:, :, NOPE_DIM:][:, :, ROPE_PERM],
         jnp.zeros((q_lora, MLA_HEADS, HEAD_PAD - QK_DIM), F32)], axis=2
    ).reshape(q_lora, MLA_HEADS * HEAD_PAD).astype(BF16)
    w_ukv_b = w_ukv[l].astype(BF16)
    gqk_q = _pad_head_vec(qk_norm_q[l])
    gqk_k = _pad_head_vec(qk_norm_k[l])
    w_s = w_spatial[l].astype(BF16)
    b_s_full = jnp.broadcast_to(b_spatial[l][:, :, None], (SGU_GROUPS, CHUNK, width // SGU_GROUPS))

    mod = _mod(cc, w_mod[l], b_mod[l])
    mod_lat = mod[:batch].reshape(batch, 1, N_MOD, d)
    sh1, sc1, g1, sh2, sc2, g2 = [mod_lat[:, :, k, :] for k in range(N_MOD)]
    sh1c = mod[batch:batch + 1, 0:d][None]
    sc1c = mod[batch:batch + 1, d:2 * d][None]

    h = _hnorm(x2d, norm1_g[l][None, :], sh1, sc1, n)
    ctx_h = _hnorm(ctx2d, norm1_g[l][None, :], sh1c, sc1c, batch * m_ctx, tm=m_ctx)

    q, k_lat, v_lat = _qkv(h, batch, w_qkv, q_norm_g[l][None, :], kv_norm_g[l][None, :], w_uq_p,
                           w_ukv_b, gqk_q, gqk_k, tabs, with_q=True)
    k_ctx, v_ctx = _qkv(ctx_h, batch, w_qkv, q_norm_g[l][None, :], kv_norm_g[l][None, :], w_uq_p,
                        w_ukv_b, gqk_q, gqk_k, tabs_ctx, with_q=False, tm=m_ctx)
    attn_o = _attention(q, k_lat, k_ctx, v_lat, v_ctx)
    sgu_o = _sgu(h, w_uv, sgu_norm_g[l][None, :], sgu_norm_b[l][None, :], w_s, b_s_full)
    merged = _merge(h, attn_o, sgu_o, w_gate, w_br_attn[l].astype(BF16), w_br_sgu[l].astype(BF16))
    x_new, h2 = _out_proj(merged, w_out[l].astype(BF16), x2d, g1, norm2_g[l][None, :], sh2, sc2, n)
    act = _ffn_in(h2, w_ffn_in[l].astype(BF16))
    y = _ffn_out(act, w_ffn_out[l].astype(BF16), x_new, g2, n)
    return y.reshape(batch, n, d)
```

```python
import functools
import math

import jax
import jax.numpy as jnp
import numpy as np
from jax import lax
from jax.experimental import pallas as pl
from jax.experimental.pallas import tpu as pltpu

F32 = jnp.float32
BF16 = jnp.bfloat16

GRID_W = 64
MLA_HEADS = 8
NOPE_DIM = 128
ROPE_DIM = 64
V_DIM = 128
QK_DIM = NOPE_DIM + ROPE_DIM
ROPE_THETA = 10000.0
CHUNK = 128
SGU_GROUPS = 8
N_MOD = 6
EPS = 1e-6

LANES = 128
HEAD_PAD = 2 * LANES
VMEM_LIMIT_BYTES = 56 * 1024 * 1024

_Q = ROPE_DIM // 4
ROPE_PERM = np.concatenate([np.arange(0, _Q), np.arange(2 * _Q, 3 * _Q),
                            np.arange(_Q, 2 * _Q), np.arange(3 * _Q, 4 * _Q)])
ROPE_SWAP = np.roll(ROPE_PERM, -ROPE_DIM // 2)


def _params(*semantics):
    return pltpu.CompilerParams(dimension_semantics=semantics,
                                vmem_limit_bytes=VMEM_LIMIT_BYTES)


def _dot(a, b):
    return jnp.dot(a, b, preferred_element_type=F32)


def _dot_nt(a, b):
    return lax.dot_general(a, b, (((1,), (1,)), ((), ())), preferred_element_type=F32)


def _mod_kernel(cc_ref, w_ref, b_ref, o_ref):
    cc = cc_ref[...]
    s = cc * jax.nn.sigmoid(cc)
    o_ref[...] = _dot(s.astype(BF16), w_ref[...].astype(BF16)) + b_ref[...]


def _mod(cc, w_mod, b_mod, tn=1024):
    rows, d = cc.shape
    n_out = w_mod.shape[1]
    return pl.pallas_call(
        _mod_kernel,
        grid=(n_out // tn,),
        in_specs=[pl.BlockSpec((rows, d), lambda j: (0, 0)),
                  pl.BlockSpec((d, tn), lambda j: (0, j)),
                  pl.BlockSpec((1, tn), lambda j: (0, j))],
        out_specs=pl.BlockSpec((rows, tn), lambda j: (0, j)),
        out_shape=jax.ShapeDtypeStruct((rows, n_out), F32),
        compiler_params=_params("arbitrary"),
        name="mod",
    )(cc, w_mod, b_mod.reshape(1, n_out))


def _rope_table_kernel(ang_ref, c_ref, s_ref):
    ang = ang_ref[...]
    lane = lax.broadcasted_iota(jnp.int32, ang.shape, 1)
    sin = jnp.sin(ang)
    c_ref[...] = jnp.where(lane < ROPE_DIM, jnp.cos(ang), 0.0)
    s_ref[...] = jnp.where(lane < ROPE_DIM // 2, -sin, jnp.where(lane < ROPE_DIM, sin, 0.0))


def _rope_tables(n):
    pos = np.arange(n)
    freqs = ROPE_THETA ** (-np.arange(_Q, dtype=np.float32) / _Q)
    ang_r = (pos // GRID_W).astype(np.float32)[:, None] * freqs[None, :]
    ang_c = (pos % GRID_W).astype(np.float32)[:, None] * freqs[None, :]
    ang = np.concatenate([ang_r, ang_c, ang_r, ang_c,
                          np.zeros((n, LANES - ROPE_DIM), np.float32)], axis=1)
    tm = 512
    spec = pl.BlockSpec((tm, LANES), lambda i: (i, 0))
    shape = jax.ShapeDtypeStruct((n, LANES), F32)
    return pl.pallas_call(
        _rope_table_kernel, grid=(n // tm,), in_specs=[spec], out_specs=[spec] * 2,
        out_shape=[shape] * 2, compiler_params=_params("arbitrary"), name="rope_tables",
    )(jnp.asarray(ang))


def _rms_rows(x, g):
    ms = jnp.mean(x * x, axis=-1, keepdims=True)
    return x * lax.rsqrt(ms + EPS) * g


def _qkv_kernel(x_ref, g1_ref, sh_ref, sc_ref, wqkv_ref, gq_ref, gkv_ref, wuq_ref, wukv_ref,
                gqk_q_ref, gqk_k_ref, c_ref, s_ref, *refs, q_lora, kv_lora, with_q):
    if with_q:
        h_ref, q_ref, k_ref, v_ref, acc_ref, kvn_ref, kva_ref, qn_ref, qa_ref = refs
    else:
        k_ref, v_ref, acc_ref, kvn_ref, kva_ref, h_ref = refs
    i = pl.program_id(0)

    @pl.when(i == 0)
    def _():
        kva_ref[1] = jnp.zeros(kva_ref.shape[1:], F32)
        acc_ref[1] = jnp.zeros(acc_ref.shape[1:], F32)
        if with_q:
            qa_ref[1] = jnp.zeros(qa_ref.shape[1:], F32)

    def step(cur, prev):
        x = x_ref[...]
        rinv = lax.rsqrt(jnp.mean(x * x, axis=-1, keepdims=True) + EPS)
        h_ref[...] = (x * rinv * (g1_ref[...] * (1.0 + sc_ref[...])) + sh_ref[...]).astype(BF16)
        acc_ref[cur] = _dot(h_ref[...], wqkv_ref[...])
        kvn_ref[...] = _rms_rows(acc_ref[cur, :, q_lora:q_lora + kv_lora],
                                 gkv_ref[...]).astype(BF16)
        kva_ref[cur] = _dot(kvn_ref[...], wukv_ref[...])
        if with_q:
            qn_ref[...] = _rms_rows(acc_ref[cur, :, :q_lora], gq_ref[...]).astype(BF16)
            qa_ref[cur] = _dot(qn_ref[...], wuq_ref[...])

        c, s = c_ref[...], s_ref[...]
        if with_q:
            g = gqk_q_ref[...] * (QK_DIM ** -0.5 * math.log2(math.e))
            g0, gc, gs = g[:, :LANES], c * g[:, LANES:2 * LANES], s * g[:, 2 * LANES:]
            for hd in range(MLA_HEADS):
                b = hd * 3 * LANES
                x0 = qa_ref[prev, :, b:b + LANES]
                x1 = qa_ref[prev, :, b + LANES:b + 2 * LANES]
                x2 = qa_ref[prev, :, b + 2 * LANES:b + 3 * LANES]
                ss = jnp.sum(x0 * x0 + x1 * x1, axis=-1, keepdims=True)
                rinv = lax.rsqrt(ss * (1.0 / QK_DIM) + EPS)
                q_ref[0, hd, :, :NOPE_DIM] = (x0 * rinv * g0).astype(BF16)
                q_ref[0, hd, :, NOPE_DIM:] = ((x1 * gc + x2 * gs) * rinv).astype(BF16)
        g = gqk_k_ref[...]
        g0 = g[:, :LANES]
        kr = q_lora + kv_lora
        r1, r2 = acc_ref[prev, :, kr:kr + LANES], acc_ref[prev, :, kr + LANES:kr + 2 * LANES]
        ss_rope = jnp.sum(r1 * r1, axis=-1, keepdims=True)
        rot = r1 * (c * g[:, LANES:2 * LANES]) + r2 * (s * g[:, 2 * LANES:])
        for hd in range(MLA_HEADS):
            b = hd * (NOPE_DIM + V_DIM)
            x0 = kva_ref[prev, :, b:b + NOPE_DIM]
            ss = jnp.sum(x0 * x0, axis=-1, keepdims=True) + ss_rope
            rinv = lax.rsqrt(ss * (1.0 / QK_DIM) + EPS)
            k_ref[0, hd, :, :NOPE_DIM] = (x0 * rinv * g0).astype(BF16)
            k_ref[0, hd, :, NOPE_DIM:] = (rot * rinv).astype(BF16)
            v_ref[0, hd] = kva_ref[prev, :, b + NOPE_DIM:b + NOPE_DIM + V_DIM].astype(BF16)

    pl.when(i % 2 == 0)(functools.partial(step, 0, 1))
    pl.when(i % 2 == 1)(functools.partial(step, 1, 0))


def _qkv(x2d, batch, g1, sh, sc, wqkv, gq, gkv, wuq, wukv, gqk_q, gqk_k, tabs, with_q, tm=256):
    t, d = x2d.shape
    n = t // batch
    tpb = n // tm
    nt = t // tm
    mod_tiles = nt // sh.shape[0]
    q_lora, kv_lora = gq.shape[1], gkv.shape[1]
    const = lambda a: pl.BlockSpec(a.shape, lambda i: (0,) * a.ndim)
    a_tile = lambda i: jnp.minimum(i, nt - 1)
    b_tile = lambda i: jnp.maximum(i - 1, 0)
    mod_spec = pl.BlockSpec((None, 1, d), lambda i: (a_tile(i) // mod_tiles, 0, 0))
    tab_spec = pl.BlockSpec((tm, LANES), lambda i: (b_tile(i) % tpb, 0))
    head_spec = lambda w: pl.BlockSpec((1, MLA_HEADS, tm, w),
                                       lambda i: (b_tile(i) // tpb, 0, b_tile(i) % tpb, 0))
    head_shape = lambda w: jax.ShapeDtypeStruct((batch, MLA_HEADS, n, w), BF16)
    out_specs = [head_spec(HEAD_PAD), head_spec(V_DIM)]
    out_shape = [head_shape(HEAD_PAD), head_shape(V_DIM)]
    scratch = [pltpu.VMEM((2, tm, wqkv.shape[1]), F32), pltpu.VMEM((tm, kv_lora), BF16),
               pltpu.VMEM((2, tm, wukv.shape[1]), F32)]
    if with_q:
        out_specs = [pl.BlockSpec((tm, d), lambda i: (a_tile(i), 0)), head_spec(HEAD_PAD)] + out_specs
        out_shape = [jax.ShapeDtypeStruct((t, d), BF16), head_shape(HEAD_PAD)] + out_shape
        scratch += [pltpu.VMEM((tm, q_lora), BF16), pltpu.VMEM((2, tm, wuq.shape[1]), F32)]
    else:
        scratch += [pltpu.VMEM((tm, d), BF16)]
    return pl.pallas_call(
        functools.partial(_qkv_kernel, q_lora=q_lora, kv_lora=kv_lora, with_q=with_q),
        grid=(nt + 1,),
        in_specs=[pl.BlockSpec((tm, d), lambda i: (a_tile(i), 0)), const(g1), mod_spec, mod_spec,
                  const(wqkv), const(gq), const(gkv), const(wuq), const(wukv),
                  const(gqk_q), const(gqk_k), tab_spec, tab_spec],
        out_specs=out_specs, out_shape=out_shape, scratch_shapes=scratch,
        compiler_params=_params("arbitrary"),
        name="qkv_latent" if with_q else "kv_context",
    )(x2d, g1, sh, sc, wqkv, gq, gkv, wuq, wukv, gqk_q, gqk_k, *tabs)


def _attn_kernel(q_ref, kl_ref, kc_ref, vl_ref, vc_ref, o_ref, s_ref, vp_ref, *, sub, kchunk):
    n, m_ctx = kl_ref.shape[2], kc_ref.shape[2]

    @pl.when(pl.program_id(2) == 0)
    def _():
        lane = lax.broadcasted_iota(jnp.int32, (n + m_ctx, LANES), 1)
        vp_ref[:n, :V_DIM] = vl_ref[0, 0]
        vp_ref[n:, :V_DIM] = vc_ref[0, 0]
        vp_ref[:, V_DIM:] = jnp.where(lane == 0, 1.0, 0.0).astype(BF16)

    chunks = [(kl_ref, j * kchunk, kchunk, j * kchunk) for j in range(n // kchunk)]
    chunks.append((kc_ref, 0, m_ctx, n))
    n_sub = q_ref.shape[2] // sub
    m_prev = None
    for t in range(n_sub + 1):
        slot = t % 2
        if t < n_sub:
            q = q_ref[0, 0, t * sub:(t + 1) * sub, :]
        m_vec = None
        acc = None
        for k_r, row0, size, col0 in chunks:
            if t < n_sub:
                s = _dot_nt(q, k_r[0, 0, row0:row0 + size, :])
                s_ref[slot, :, col0:col0 + size] = s
                for j in range(size // LANES):
                    tile = s[:, j * LANES:(j + 1) * LANES]
                    m_vec = tile if m_vec is None else jnp.maximum(m_vec, tile)
            if t >= 1:
                p = jnp.exp2(s_ref[1 - slot, :, col0:col0 + size] - m_prev).astype(BF16)
                d = _dot(p, vp_ref[col0:col0 + size, :])
                acc = d if acc is None else acc + d
        if t >= 1:
            o = acc[:, :V_DIM] / acc[:, V_DIM:V_DIM + 1]
            o_ref[(t - 1) * sub:t * sub, :] = o.astype(o_ref.dtype)
        if t < n_sub:
            m_prev = jnp.max(m_vec, axis=-1, keepdims=True)


def _attention(q, k_lat, k_ctx, v_lat, v_ctx, tq=1024, sub=256, kchunk=512):
    b, hds, n, _ = q.shape
    m_ctx = k_ctx.shape[2]
    nq = n // tq
    kv_spec = lambda rows, w: pl.BlockSpec((1, 1, rows, w), lambda bi, hi, i: (bi, hi, 0, 0))
    return pl.pallas_call(
        functools.partial(_attn_kernel, sub=sub, kchunk=kchunk),
        grid=(b, hds, nq),
        in_specs=[pl.BlockSpec((1, 1, tq, HEAD_PAD), lambda bi, hi, i: (bi, hi, i, 0)),
                  kv_spec(n, HEAD_PAD), kv_spec(m_ctx, HEAD_PAD),
                  kv_spec(n, V_DIM), kv_spec(m_ctx, V_DIM)],
        out_specs=pl.BlockSpec((tq, V_DIM), lambda bi, hi, i: (bi * nq + i, hi)),
        out_shape=jax.ShapeDtypeStruct((b * n, hds * V_DIM), BF16),
        scratch_shapes=[pltpu.VMEM((2, sub, n + m_ctx), F32),
                        pltpu.VMEM((n + m_ctx, 2 * V_DIM), BF16)],
        compiler_params=_params("arbitrary", "arbitrary", "arbitrary"),
        name="attention",
    )(q, k_lat, k_ctx, v_lat, v_ctx)


def _gelu(x):
    return 0.5 * x * (1.0 + lax.erf(x * (2.0 ** -0.5)))


def _sgu_kernel(h_ref, w_ref, g_ref, b_ref, ws_ref, bs_ref, o_ref, *, width):
    acc = _dot(h_ref[...], w_ref[...])
    u = _gelu(acc[:, :width])
    v = _gelu(acc[:, width:])
    mu = jnp.mean(v, axis=-1, keepdims=True)
    vc = v - mu
    var = jnp.mean(vc * vc, axis=-1, keepdims=True)
    vn = (vc * lax.rsqrt(var + EPS) * g_ref[...] + b_ref[...]).astype(BF16)
    gd = width // SGU_GROUPS
    for ck in range(h_ref.shape[0] // CHUNK):
        rows = slice(ck * CHUNK, (ck + 1) * CHUNK)
        for g in range(SGU_GROUPS):
            cols = slice(g * gd, (g + 1) * gd)
            mixed = _dot(ws_ref[g], vn[rows, cols]) + bs_ref[g]
            o_ref[rows, cols] = (u[rows, cols] * mixed).astype(o_ref.dtype)


def _sgu(h2d, w_uv, norm_g, norm_b, w_s, b_s_full, tm=512):
    t, d = h2d.shape
    width = w_uv.shape[1] // 2
    const = lambda a: pl.BlockSpec(a.shape, lambda i: (0,) * a.ndim)
    return pl.pallas_call(
        functools.partial(_sgu_kernel, width=width),
        grid=(t // tm,),
        in_specs=[pl.BlockSpec((tm, d), lambda i: (i, 0)), const(w_uv), const(norm_g),
                  const(norm_b), const(w_s), const(b_s_full)],
        out_specs=pl.BlockSpec((tm, width), lambda i: (i, 0)),
        out_shape=jax.ShapeDtypeStruct((t, width), BF16),
        compiler_params=_params("arbitrary"),
        name="sgu",
    )(h2d, w_uv, norm_g, norm_b, w_s, b_s_full)


def _merge_kernel(h_ref, a_ref, s_ref, wga_ref, wgs_ref, wa_ref, ws_ref, o_ref):
    h = h_ref[...]
    ga = jax.nn.sigmoid(_dot(h, wga_ref[...]))
    gs = jax.nn.sigmoid(_dot(h, wgs_ref[...]))
    merged = ga * _dot(a_ref[...], wa_ref[...]) + gs * _dot(s_ref[...], ws_ref[...])
    o_ref[...] = merged.astype(o_ref.dtype)


def _merge(h2d, attn_o, sgu_o, w_gate, w_br_attn, w_br_sgu, tm=512, tn=512):
    t, d = h2d.shape
    d_out = w_br_attn.shape[1]
    nj = d_out // tn
    row = lambda w: pl.BlockSpec((tm, w), lambda i, j: (i, 0))
    return pl.pallas_call(
        _merge_kernel,
        grid=(t // tm, nj),
        in_specs=[row(d), row(attn_o.shape[1]), row(sgu_o.shape[1]),
                  pl.BlockSpec((d, tn), lambda i, j: (0, j)),
                  pl.BlockSpec((d, tn), lambda i, j: (0, j + nj)),
                  pl.BlockSpec((w_br_attn.shape[0], tn), lambda i, j: (0, j)),
                  pl.BlockSpec((w_br_sgu.shape[0], tn), lambda i, j: (0, j))],
        out_specs=pl.BlockSpec((tm, tn), lambda i, j: (i, j)),
        out_shape=jax.ShapeDtypeStruct((t, d_out), BF16),
        compiler_params=_params("arbitrary", "arbitrary"),
        name="merge",
    )(h2d, attn_o, sgu_o, w_gate, w_gate, w_br_attn, w_br_sgu)


def _out_kernel(m_ref, w_ref, x_ref, g1_ref, g_ref, sh_ref, sc_ref, xn_ref, h2_ref):
    xn = x_ref[...] + g1_ref[...] * _dot(m_ref[...], w_ref[...])
    xn_ref[...] = xn
    h2_ref[...] = (_rms_rows(xn, g_ref[...]) * (1.0 + sc_ref[...]) + sh_ref[...]).astype(BF16)


def _out_proj(merged, w_out, x2d, g1, norm2_g, sh2, sc2, rows_per_mod, tm=256):
    t, d = x2d.shape
    tpb = rows_per_mod // tm
    mod_spec = pl.BlockSpec((None, 1, d), lambda i: (i // tpb, 0, 0))
    row = pl.BlockSpec((tm, d), lambda i: (i, 0))
    return pl.pallas_call(
        _out_kernel,
        grid=(t // tm,),
        in_specs=[row, pl.BlockSpec(w_out.shape, lambda i: (0, 0)), row, mod_spec,
                  pl.BlockSpec((1, d), lambda i: (0, 0)), mod_spec, mod_spec],
        out_specs=[row, row],
        out_shape=[jax.ShapeDtypeStruct((t, d), F32), jax.ShapeDtypeStruct((t, d), BF16)],
        compiler_params=_params("arbitrary"),
        name="out_proj",
    )(merged, w_out, x2d, g1, norm2_g, sh2, sc2)


def _ffn_in_kernel(h_ref, wa_ref, wb_ref, o_ref):
    h = h_ref[...]
    a = _dot(h, wa_ref[...])
    b = _dot(h, wb_ref[...])
    o_ref[...] = (a * jax.nn.sigmoid(a) * b).astype(o_ref.dtype)


def _ffn_in(h2, w_ffn_in, tm=1024, tn=512):
    t, d = h2.shape
    d_ff = w_ffn_in.shape[1] // 2
    nj = d_ff // tn
    return pl.pallas_call(
        _ffn_in_kernel,
        grid=(t // tm, nj),
        in_specs=[pl.BlockSpec((tm, d), lambda i, j: (i, 0)),
                  pl.BlockSpec((d, tn), lambda i, j: (0, j)),
                  pl.BlockSpec((d, tn), lambda i, j: (0, j + nj))],
        out_specs=pl.BlockSpec((tm, tn), lambda i, j: (i, j)),
        out_shape=jax.ShapeDtypeStruct((t, d_ff), BF16),
        compiler_params=_params("arbitrary", "arbitrary"),
        name="ffn_in",
    )(h2, w_ffn_in, w_ffn_in)


def _ffn_out_kernel(a_ref, w_ref, x_ref, g2_ref, o_ref):
    o_ref[...] = x_ref[...] + g2_ref[...] * _dot(a_ref[...], w_ref[...])


def _ffn_out(act, w_ffn_out, x_new, g2, rows_per_mod, tm=512, tn=512):
    t, d_ff = act.shape
    d = w_ffn_out.shape[1]
    tpb = rows_per_mod // tm
    return pl.pallas_call(
        _ffn_out_kernel,
        grid=(t // tm, d // tn),
        in_specs=[pl.BlockSpec((tm, d_ff), lambda i, j: (i, 0)),
                  pl.BlockSpec((d_ff, tn), lambda i, j: (0, j)),
                  pl.BlockSpec((tm, tn), lambda i, j: (i, j)),
                  pl.BlockSpec((None, 1, tn), lambda i, j: (i // tpb, 0, j))],
        out_specs=pl.BlockSpec((tm, tn), lambda i, j: (i, j)),
        out_shape=jax.ShapeDtypeStruct((t, d), F32),
        compiler_params=_params("arbitrary", "arbitrary"),
        name="ffn_out",
    )(act, w_ffn_out, x_new, g2)


def _rope_pair(w):
    pad = jnp.zeros(w.shape[:-1] + (LANES - ROPE_DIM,), w.dtype)
    return jnp.concatenate([w[..., ROPE_PERM], pad, w[..., ROPE_SWAP], pad], axis=-1)


def _head_vec(g):
    return jnp.concatenate([g[:NOPE_DIM], _rope_pair(g[NOPE_DIM:])])[None, :]


def kernel(x, c, ctx, c_ctx, w_mod, b_mod, norm1_g, w_in, q_norm_g, kv_norm_g, w_uq, w_ukv,
           qk_norm_q, qk_norm_k, sgu_norm_g, sgu_norm_b, w_spatial, b_spatial, w_br_attn,
           w_br_sgu, w_out, norm2_g, w_ffn_in, w_ffn_out):
    batch, n, d = x.shape
    m_ctx = ctx.shape[1]
    depth = w_mod.shape[0]
    q_lora, kv_lora = q_norm_g.shape[1], kv_norm_g.shape[1]
    width = sgu_norm_g.shape[1]
    off_u = q_lora + kv_lora + ROPE_DIM
    off_gate = off_u + 2 * width
    assert depth == 1 and n % GRID_W == 0

    tabs = _rope_tables(n)
    zeros = jnp.zeros((m_ctx, LANES), F32)
    tabs_ctx = (jnp.where(jnp.arange(LANES) < ROPE_DIM, 1.0, 0.0).astype(F32)[None, :] + zeros,
                zeros)

    mod_rows = 8
    cc = jnp.concatenate([c, c_ctx[None, :], jnp.zeros((mod_rows - batch - 1, d), F32)], axis=0)
    x2d = x.reshape(batch * n, d)
    ctx2d = ctx.reshape(batch * m_ctx, d)

    l = 0
    w_in_l = w_in[l]
    w_qkv = jnp.concatenate(
        [w_in_l[:, :q_lora + kv_lora], _rope_pair(w_in_l[:, q_lora + kv_lora:off_u])],
        axis=1).astype(BF16)
    w_uv = w_in_l[:, off_u:off_gate].astype(BF16)
    w_gate = w_in_l[:, off_gate:].astype(BF16)
    wq3 = w_uq[l].reshape(q_lora, MLA_HEADS, QK_DIM)
    w_uq_p = jnp.concatenate([wq3[:, :, :NOPE_DIM], _rope_pair(wq3[:, :, NOPE_DIM:])], axis=2
                             ).reshape(q_lora, MLA_HEADS * 3 * LANES).astype(BF16)
    w_ukv_b = w_ukv[l].astype(BF16)
    gqk_q = _head_vec(qk_norm_q[l])
    gqk_k = _head_vec(qk_norm_k[l])
    w_s = w_spatial[l].astype(BF16)
    b_s_full = jnp.broadcast_to(b_spatial[l][:, :, None], (SGU_GROUPS, CHUNK, width // SGU_GROUPS))

    mod = _mod(cc, w_mod[l], b_mod[l])
    mod_lat = mod[:batch].reshape(batch, 1, N_MOD, d)
    sh1, sc1, g1, sh2, sc2, g2 = [mod_lat[:, :, k, :] for k in range(N_MOD)]
    sh1c = mod[batch:batch + 1, 0:d][None]
    sc1c = mod[batch:batch + 1, d:2 * d][None]

    qkv_weights = (w_qkv, q_norm_g[l][None, :], kv_norm_g[l][None, :], w_uq_p, w_ukv_b,
                   gqk_q, gqk_k)
    h, q, k_lat, v_lat = _qkv(x2d, batch, norm1_g[l][None, :], sh1, sc1, *qkv_weights, tabs,
                              with_q=True)
    k_ctx, v_ctx = _qkv(ctx2d, batch, norm1_g[l][None, :], sh1c, sc1c, *qkv_weights, tabs_ctx,
                        with_q=False)
    attn_o = _attention(q, k_lat, k_ctx, v_lat, v_ctx)
    sgu_o = _sgu(h, w_uv, sgu_norm_g[l][None, :], sgu_norm_b[l][None, :], w_s, b_s_full)
    merged = _merge(h, attn_o, sgu_o, w_gate, w_br_attn[l].astype(BF16), w_br_sgu[l].astype(BF16))
    x_new, h2 = _out_proj(merged, w_out[l].astype(BF16), x2d, g1, norm2_g[l][None, :], sh2, sc2, n)
    act = _ffn_in(h2, w_ffn_in[l].astype(BF16))
    y = _ffn_out(act, w_ffn_out[l].astype(BF16), x_new, g2, n)
    return y.reshape(batch, n, d)
```
